```python
import math
import jax, jax.numpy as jnp
from jax import lax
import numpy as np

D_MODEL = 1024
BATCH = 16
SEQ = 2048
DEPTH = 4
DEC_BATCH = 16
DEC_SEQ = 4096
PAST_LEN = 128

N_MIXERS = 2
N_A_LAYERS = (DEPTH + 1) // 2
N_B_LAYERS = DEPTH // 2
HEAD_DIM = 64
ROPE_THETA = 10000.0
RMS_EPS = 1e-6
BLOCK = 128
A_HEADS = D_MODEL // HEAD_DIM
A_KV_HEADS = A_HEADS // 4
A_GROUP = A_HEADS // A_KV_HEADS
A_Q_DIM = A_HEADS * HEAD_DIM
A_KV_DIM = A_KV_HEADS * HEAD_DIM
A_IN_DIM = A_Q_DIM + 2 * A_KV_DIM
WINDOW = 128
A_SPAN = BLOCK + 2 * WINDOW
B_HEADS = D_MODEL // (2 * HEAD_DIM)
B_QK_DIM = B_HEADS * 2 * HEAD_DIM
B_V_DIM = B_HEADS * 2 * HEAD_DIM
B_IN_DIM = 2 * B_QK_DIM + B_V_DIM
D_FF = int(math.ceil(8 * D_MODEL / 3 / 256) * 256)

kernel_name = "hybrid_swa_sink_diffattn_encoder"


def rms_norm(x, g):
    xf = x.astype(jnp.float32)
    y = xf * lax.rsqrt(jnp.mean(xf * xf, axis=-1, keepdims=True) + RMS_EPS)
    return (y * g.astype(jnp.float32)).astype(x.dtype)


def rope_tables(seq, dim):
    inv = 1.0 / (ROPE_THETA ** (jnp.arange(0, dim, 2, dtype=jnp.float32) / dim))
    ang = jnp.arange(seq, dtype=jnp.float32)[:, None] * inv[None, :]
    return jnp.cos(ang), jnp.sin(ang)


def apply_rope(x, cos, sin):
    x1, x2 = jnp.split(x, 2, axis=-1)
    c = cos[None, :, None, :].astype(x.dtype)
    s = sin[None, :, None, :].astype(x.dtype)
    return jnp.concatenate([x1 * c - x2 * s, x2 * c + x1 * s], axis=-1)


def window_gqa_sink(h, w_in, w_o, sinks):
    b, s, _ = h.shape
    n_blk = s // BLOCK
    qkv = h @ w_in
    q, k, v = jnp.split(qkv, [A_Q_DIM, A_Q_DIM + A_KV_DIM], axis=-1)
    q = q.reshape(b, s, A_HEADS, HEAD_DIM)
    k = k.reshape(b, s, A_KV_HEADS, HEAD_DIM)
    v = v.reshape(b, s, A_KV_HEADS, HEAD_DIM)
    cos, sin = rope_tables(s, HEAD_DIM)
    q = apply_rope(q, cos, sin) * (HEAD_DIM ** -0.5)
    k = apply_rope(k, cos, sin)
    q = q.reshape(b, n_blk, BLOCK, A_KV_HEADS, A_GROUP, HEAD_DIM).transpose(1, 0, 2, 3, 4, 5)
    pad = ((0, 0), (WINDOW, WINDOW), (0, 0), (0, 0))
    kp = jnp.pad(k, pad)
    vp = jnp.pad(v, pad)
    sink = sinks.astype(jnp.float32).reshape(1, A_KV_HEADS, A_GROUP, 1, 1)

    def block(args):
        qi, i = args
        start = i * BLOCK
        kb = lax.dynamic_slice_in_dim(kp, start, A_SPAN, axis=1)
        vb = lax.dynamic_slice_in_dim(vp, start, A_SPAN, axis=1)
        sc = jnp.einsum('bqkgd,bskd->bkgqs', qi, kb).astype(jnp.float32)
        qpos = start + jnp.arange(BLOCK)
        kpos = start - WINDOW + jnp.arange(A_SPAN)
        valid = (jnp.abs(qpos[:, None] - kpos[None, :]) <= WINDOW) & (kpos >= 0)[None, :] & (kpos < s)[None, :]
        sc = jnp.where(valid, sc, -jnp.inf)
        m = jnp.maximum(jnp.max(sc, axis=-1, keepdims=True), sink)
        e = jnp.exp(sc - m)
        p = e / (jnp.sum(e, axis=-1, keepdims=True) + jnp.exp(sink - m))
        return jnp.einsum('bkgqs,bskd->bqkgd', p.astype(vb.dtype), vb)

    o = lax.map(block, (q, jnp.arange(n_blk)))
    o = o.transpose(1, 0, 2, 3, 4, 5).reshape(b, s, A_Q_DIM)
    return o @ w_o


def diff_attention(h, w_in, w_o, lq1, lk1, lq2, lk2, subln_g, lambda_init):
    b, s, _ = h.shape
    n_blk = s // BLOCK
    qkv = h @ w_in
    q, k, v = jnp.split(qkv, [B_QK_DIM, 2 * B_QK_DIM], axis=-1)
    q = q.reshape(b, s, 2 * B_HEADS, HEAD_DIM)
    k = k.reshape(b, s, 2 * B_HEADS, HEAD_DIM)
    v = v.reshape(b, s, B_HEADS, 2 * HEAD_DIM)
    cos, sin = rope_tables(s, HEAD_DIM)
    q = (apply_rope(q, cos, sin) * (HEAD_DIM ** -0.5)).reshape(b, s, B_HEADS, 2, HEAD_DIM)
    k = apply_rope(k, cos, sin).reshape(b, s, B_HEADS, 2, HEAD_DIM)
    lam = (jnp.exp(jnp.sum(lq1.astype(jnp.float32) * lk1.astype(jnp.float32)))
           - jnp.exp(jnp.sum(lq2.astype(jnp.float32) * lk2.astype(jnp.float32)))
           + lambda_init)
    qb = q.reshape(b, n_blk, BLOCK, B_HEADS, 2, HEAD_DIM).transpose(1, 0, 2, 3, 4, 5)

    def block(qi):
        sc = jnp.einsum('bqhcd,bkhcd->bhcqk', qi, k).astype(jnp.float32)
        p = jax.nn.softmax(sc, axis=-1)
        a = p[:, :, 0] - lam * p[:, :, 1]
        return jnp.einsum('bhqk,bkhe->bqhe', a.astype(v.dtype), v)

    o = lax.map(block, qb)
    o = o.transpose(1, 0, 2, 3, 4).reshape(b, s, B_HEADS, 2 * HEAD_DIM)
    o = rms_norm(o, subln_g) * (1.0 - lambda_init)
    return o.reshape(b, s, B_V_DIM) @ w_o


def swiglu(h, w_gate_up, w_down):
    g, u = jnp.split(h @ w_gate_up, 2, axis=-1)
    return (jax.nn.silu(g) * u) @ w_down


def trunk(x, norm_mix_pre, norm_mix_post, norm_ffn_pre, norm_ffn_post,
          a_w_in, a_w_o, a_sinks, b_w_in, b_w_o, b_lambda_q1, b_lambda_k1,
          b_lambda_q2, b_lambda_k2, b_subln, ffn_w_gate_up, ffn_w_down):
    for i in range(DEPTH):
        h = rms_norm(x, norm_mix_pre[i])
        j = i // N_MIXERS
        if i % N_MIXERS == 0:
            m = window_gqa_sink(h, a_w_in[j], a_w_o[j], a_sinks[j])
        else:
            lambda_init = 0.8 - 0.6 * math.exp(-0.3 * i)
            m = diff_attention(h, b_w_in[j], b_w_o[j], b_lambda_q1[j], b_lambda_k1[j],
                               b_lambda_q2[j], b_lambda_k2[j], b_subln[j], lambda_init)
        x = x + rms_norm(m, norm_mix_post[i])
        h = rms_norm(x, norm_ffn_pre[i])
        x = x + rms_norm(swiglu(h, ffn_w_gate_up[i], ffn_w_down[i]), norm_ffn_post[i])
    return x


def setup_inputs(seed: int = 0) -> dict:
    key = jax.random.key(seed)
    ks = jax.random.split(key, 20)
    f32 = jnp.float32

    def w(k, shape, fan_in):
        return jax.random.normal(k, shape, f32) * (fan_in ** -0.5)

    def gain(k, shape):
        return 1.0 + 0.05 * jax.random.normal(k, shape, f32)

    return {
        "x_prompt": jax.random.normal(ks[0], (BATCH, SEQ, D_MODEL), f32),
        "x_sample": jax.random.normal(ks[1], (DEC_BATCH, DEC_SEQ, D_MODEL), f32),
        "norm_mix_pre": gain(ks[2], (DEPTH, D_MODEL)),
        "norm_mix_post": gain(ks[3], (DEPTH, D_MODEL)),
        "norm_ffn_pre": gain(ks[4], (DEPTH, D_MODEL)),
        "norm_ffn_post": gain(ks[5], (DEPTH, D_MODEL)),
        "a_w_in": w(ks[6], (N_A_LAYERS, D_MODEL, A_IN_DIM), D_MODEL),
        "a_w_o": w(ks[7], (N_A_LAYERS, A_Q_DIM, D_MODEL), A_Q_DIM),
        "a_sinks": jax.random.normal(ks[8], (N_A_LAYERS, A_HEADS), f32),
        "b_w_in": w(ks[9], (N_B_LAYERS, D_MODEL, B_IN_DIM), D_MODEL),
        "b_w_o": w(ks[10], (N_B_LAYERS, B_V_DIM, D_MODEL), B_V_DIM),
        "b_lambda_q1": 0.1 * jax.random.normal(ks[11], (N_B_LAYERS, HEAD_DIM), f32),
        "b_lambda_k1": 0.1 * jax.random.normal(ks[12], (N_B_LAYERS, HEAD_DIM), f32),
        "b_lambda_q2": 0.1 * jax.random.normal(ks[13], (N_B_LAYERS, HEAD_DIM), f32),
        "b_lambda_k2": 0.1 * jax.random.normal(ks[14], (N_B_LAYERS, HEAD_DIM), f32),
        "b_subln": gain(ks[15], (N_B_LAYERS, 2 * HEAD_DIM)),
        "ffn_w_gate_up": w(ks[16], (DEPTH, D_MODEL, 2 * D_FF), D_MODEL),
        "ffn_w_down": w(ks[17], (DEPTH, D_FF, D_MODEL), D_FF),
    }


def reference(x_prompt, x_sample, norm_mix_pre, norm_mix_post, norm_ffn_pre, norm_ffn_post,
              a_w_in, a_w_o, a_sinks, b_w_in, b_w_o, b_lambda_q1, b_lambda_k1,
              b_lambda_q2, b_lambda_k2, b_subln, ffn_w_gate_up, ffn_w_down):
    y_prompt = trunk(x_prompt, norm_mix_pre, norm_mix_post, norm_ffn_pre, norm_ffn_post,
                     a_w_in, a_w_o, a_sinks, b_w_in, b_w_o, b_lambda_q1, b_lambda_k1,
                     b_lambda_q2, b_lambda_k2, b_subln, ffn_w_gate_up, ffn_w_down)
    y_sample = trunk(x_sample, norm_mix_pre, norm_mix_post, norm_ffn_pre, norm_ffn_post,
                     a_w_in, a_w_o, a_sinks, b_w_in, b_w_o, b_lambda_q1, b_lambda_k1,
                     b_lambda_q2, b_lambda_k2, b_subln, ffn_w_gate_up, ffn_w_down)
    return (y_prompt, y_sample)
```

```python
import functools
import math

import jax
import jax.numpy as jnp
from jax import lax
from jax.experimental import pallas as pl
from jax.experimental.pallas import tpu as pltpu

D_MODEL = 1024
DEPTH = 4
HEAD_DIM = 64
ROPE_THETA = 10000.0
RMS_EPS = 1e-6
WINDOW = 128
A_HEADS = 16
A_KV_HEADS = 4
A_GROUP = A_HEADS // A_KV_HEADS
A_Q_DIM = A_HEADS * HEAD_DIM
A_KV_DIM = A_KV_HEADS * HEAD_DIM
B_HEADS = 8
B_QK_DIM = B_HEADS * 2 * HEAD_DIM
B_V_DIM = B_HEADS * 2 * HEAD_DIM
D_FF = 2816

LANES = 128
VMEM_LIMIT = 56 * 1024 * 1024

TOKEN_TILE = 512
QKV_COL_CHUNK = 512
FFN_CHUNKS = ((0, 1024), (1024, 1024), (2048, 768))
WIN_Q_TILE = 128
WIN_SPAN = WIN_Q_TILE + 2 * WINDOW
DIFF_Q_TILE = 256
DIFF_KV_TILE = 512

_NT = (((1,), (1,)), ((), ()))

f32 = jnp.float32
bf16 = jnp.bfloat16


def _rms(x, g):
    return x * lax.rsqrt(jnp.mean(x * x, axis=-1, keepdims=True) + RMS_EPS) * g


def _const_spec(shape):
    return pl.BlockSpec(shape, lambda *_: (0,) * len(shape), pipeline_mode=pl.Buffered(1))


def _qkv_kernel(x_ref, g_ref, w_ref, cos_ref, sin_ref, q_ref, k_ref, v_ref, *, q_dim, k_dim):
    h = _rms(x_ref[...], g_ref[...]).astype(bf16)
    cos = cos_ref[...]
    sin = sin_ref[...]
    lane = lax.broadcasted_iota(jnp.int32, cos.shape, 1)
    first_half = (lane % HEAD_DIM) < (HEAD_DIM // 2)
    n_total = w_ref.shape[1]
    for c0 in range(0, n_total, QKV_COL_CHUNK):
        y = jnp.dot(h, w_ref[:, c0:c0 + QKV_COL_CHUNK], preferred_element_type=f32)
        for l0 in range(0, QKV_COL_CHUNK, LANES):
            col = c0 + l0
            yc = y[:, l0:l0 + LANES]
            if col < q_dim + k_dim:
                partner = jnp.where(first_half,
                                    pltpu.roll(yc, LANES - HEAD_DIM // 2, 1),
                                    pltpu.roll(yc, HEAD_DIM // 2, 1))
                yc = yc * cos + partner * sin
            if col < q_dim:
                q_ref[:, col:col + LANES] = (yc * (HEAD_DIM ** -0.5)).astype(bf16)
            elif col < q_dim + k_dim:
                k_ref[:, col - q_dim:col - q_dim + LANES] = yc.astype(bf16)
            else:
                v_ref[:, col - q_dim - k_dim:col - q_dim - k_dim + LANES] = yc.astype(bf16)


def _qkv_call(x, gain, w, cos, sin, *, seq, q_dim, k_dim, v_dim):
    t = x.shape[0]
    tm = TOKEN_TILE
    pos_blocks = seq // tm
    kern = functools.partial(_qkv_kernel, q_dim=q_dim, k_dim=k_dim)
    return pl.pallas_call(
        kern,
        grid=(t // tm,),
        in_specs=[
            pl.BlockSpec((tm, D_MODEL), lambda i: (i, 0)),
            _const_spec((1, D_MODEL)),
            _const_spec(w.shape),
            pl.BlockSpec((tm, LANES), lambda i: (i % pos_blocks, 0)),
            pl.BlockSpec((tm, LANES), lambda i: (i % pos_blocks, 0)),
        ],
        out_specs=[
            pl.BlockSpec((tm, q_dim), lambda i: (i, 0)),
            pl.BlockSpec((tm, k_dim), lambda i: (i, 0)),
            pl.BlockSpec((tm, v_dim), lambda i: (i, 0)),
        ],
        out_shape=[
            jax.ShapeDtypeStruct((t, q_dim), bf16),
            jax.ShapeDtypeStruct((t, k_dim), bf16),
            jax.ShapeDtypeStruct((t, v_dim), bf16),
        ],
        compiler_params=pltpu.CompilerParams(
            dimension_semantics=("parallel",), vmem_limit_bytes=VMEM_LIMIT),
        name="qkv_rope",
    )(x, gain, w, cos, sin)


def _win_attn_kernel(sink_ref, q_ref, k_ref, v_ref, o_ref, *, seq):
    i = pl.program_id(1)
    n_blk = seq // WIN_Q_TILE
    first = jnp.clip(i - 1, 0, n_blk - WIN_SPAN // WIN_Q_TILE)
    start = pl.multiple_of(first * WIN_Q_TILE, WIN_Q_TILE)
    kb = k_ref[pl.ds(start, WIN_SPAN), :]
    vb = v_ref[pl.ds(start, WIN_SPAN), :]
    rows = A_GROUP * WIN_Q_TILE
    qpos = i * WIN_Q_TILE + lax.broadcasted_iota(jnp.int32, (rows, WIN_SPAN), 0) % WIN_Q_TILE
    kpos = start + lax.broadcasted_iota(jnp.int32, (rows, WIN_SPAN), 1)
    valid = jnp.abs(qpos - kpos) <= WINDOW
    q = q_ref[...]
    outs = []
    for kvh in range(A_KV_HEADS):
        kh = kb[:, kvh * HEAD_DIM:(kvh + 1) * HEAD_DIM]
        vh = vb[:, kvh * HEAD_DIM:(kvh + 1) * HEAD_DIM]
        heads = [kvh * A_GROUP + g for g in range(A_GROUP)]
        qs = jnp.concatenate([q[:, h * HEAD_DIM:(h + 1) * HEAD_DIM] for h in heads], axis=0)
        sink = jnp.concatenate(
            [jnp.full((WIN_Q_TILE, 1), sink_ref[h], f32) for h in heads], axis=0)
        s = lax.dot_general(qs, kh, _NT, preferred_element_type=f32)
        s = jnp.where(valid, s, -jnp.inf)
        m = jnp.maximum(jnp.max(s, axis=-1, keepdims=True), sink)
        e = jnp.exp(s - m)
        denom = jnp.sum(e, axis=-1, keepdims=True) + jnp.exp(sink - m)
        o = jnp.dot(e.astype(bf16), vh, preferred_element_type=f32) / denom
        outs.extend(o[g * WIN_Q_TILE:(g + 1) * WIN_Q_TILE] for g in range(A_GROUP))
    o_ref[...] = jnp.concatenate(outs, axis=-1).astype(bf16)


def _win_attn_call(q, k, v, sinks, *, batch, seq):
    n_blk = seq // WIN_Q_TILE
    kern = functools.partial(_win_attn_kernel, seq=seq)
    return pl.pallas_call(
        kern,
        grid=(batch, n_blk),
        in_specs=[
            pl.BlockSpec(memory_space=pltpu.SMEM),
            pl.BlockSpec((WIN_Q_TILE, A_Q_DIM), lambda b, i: (b * n_blk + i, 0)),
            pl.BlockSpec((seq, A_KV_DIM), lambda b, i: (b, 0)),
            pl.BlockSpec((seq, A_KV_DIM), lambda b, i: (b, 0)),
        ],
        out_specs=pl.BlockSpec((WIN_Q_TILE, A_Q_DIM), lambda b, i: (b * n_blk + i, 0)),
        out_shape=jax.ShapeDtypeStruct(q.shape, bf16),
        compiler_params=pltpu.CompilerParams(
            dimension_semantics=("parallel", "arbitrary"), vmem_limit_bytes=VMEM_LIMIT),
        name="win_attn",
    )(sinks, q, k, v)


def _diff_attn_kernel(q_ref, k_ref, v_ref, lq1_ref, lk1_ref, lq2_ref, lk2_ref, g_ref, o_ref,
                      m_sc, l_sc, acc_sc, *, seq, lambda_init):
    tq = DIFF_Q_TILE
    tk = DIFF_KV_TILE
    q2 = q_ref[...]
    lane = lax.broadcasted_iota(jnp.int32, q2.shape, 1)
    zero = jnp.zeros_like(q2)
    qs = jnp.concatenate([jnp.where(lane < HEAD_DIM, q2, zero),
                          jnp.where(lane >= HEAD_DIM, q2, zero)], axis=0)
    m_sc[...] = jnp.full(m_sc.shape, -jnp.inf, f32)
    l_sc[...] = jnp.zeros(l_sc.shape, f32)
    acc_sc[...] = jnp.zeros(acc_sc.shape, f32)

    def body(j, carry):
        off = pl.multiple_of(j * tk, tk)
        kt = k_ref[pl.ds(off, tk), :]
        vt = v_ref[pl.ds(off, tk), :]
        s = lax.dot_general(qs, kt, _NT, preferred_element_type=f32)
        m_prev = m_sc[...]
        m_new = jnp.maximum(m_prev, jnp.max(s, axis=-1, keepdims=True))
        alpha = jnp.exp(m_prev - m_new)
        p = jnp.exp(s - m_new)
        l_sc[...] = alpha * l_sc[...] + jnp.sum(p, axis=-1, keepdims=True)
        acc_sc[...] = alpha * acc_sc[...] + jnp.dot(p.astype(bf16), vt,
                                                    preferred_element_type=f32)
        m_sc[...] = m_new
        return carry

    lax.fori_loop(0, seq // tk, body, 0)

    lam = (jnp.exp(jnp.sum(lq1_ref[...] * lk1_ref[...], axis=-1, keepdims=True))
           - jnp.exp(jnp.sum(lq2_ref[...] * lk2_ref[...], axis=-1, keepdims=True))
           + lambda_init)
    o_all = acc_sc[...] / l_sc[...]
    o = o_all[:tq] - lam * o_all[tq:]
    o_ref[...] = (_rms(o, g_ref[...]) * (1.0 - lambda_init)).astype(bf16)


def _diff_attn_call(q, k, v, lq1, lk1, lq2, lk2, subln_g, *, batch, seq, lambda_init):
    tq = DIFF_Q_TILE
    n_q = seq // tq
    kern = functools.partial(_diff_attn_kernel, seq=seq, lambda_init=lambda_init)
    vec = _const_spec((1, HEAD_DIM))
    return pl.pallas_call(
        kern,
        grid=(batch, B_HEADS, n_q),
        in_specs=[
            pl.BlockSpec((tq, LANES), lambda b, h, i: (b * n_q + i, h)),
            pl.BlockSpec((seq, LANES), lambda b, h, i: (b, h)),
            pl.BlockSpec((seq, LANES), lambda b, h, i: (b, h)),
            vec, vec, vec, vec,
            _const_spec((1, 2 * HEAD_DIM)),
        ],
        out_specs=pl.BlockSpec((tq, LANES), lambda b, h, i: (b * n_q + i, h)),
        out_shape=jax.ShapeDtypeStruct(q.shape, bf16),
        scratch_shapes=[
            pltpu.VMEM((2 * tq, 1), f32),
            pltpu.VMEM((2 * tq, 1), f32),
            pltpu.VMEM((2 * tq, 2 * HEAD_DIM), f32),
        ],
        compiler_params=pltpu.CompilerParams(
            dimension_semantics=("parallel", "parallel", "arbitrary"),
            vmem_limit_bytes=VMEM_LIMIT),
        name="diff_attn",
    )(q, k, v, lq1, lk1, lq2, lk2, subln_g)


def _proj_ffn_kernel(o_ref, x_ref, wo_ref, g_post_ref, g_pre_ref, wgu_ref, wd_ref, g_fpost_ref,
                     y_ref):
    mixed = jnp.dot(o_ref[...], wo_ref[...], preferred_element_type=f32)
    x1 = x_ref[...] + _rms(mixed, g_post_ref[...])
    h = _rms(x1, g_pre_ref[...]).astype(bf16)
    acc = None
    for c0, width in FFN_CHUNKS:
        gate = jnp.dot(h, wgu_ref[:, c0:c0 + width], preferred_element_type=f32)
        up = jnp.dot(h, wgu_ref[:, D_FF + c0:D_FF + c0 + width], preferred_element_type=f32)
        act = (gate * jax.nn.sigmoid(gate) * up).astype(bf16)
        part = jnp.dot(act, wd_ref[c0:c0 + width, :], preferred_element_type=f32)
        acc = part if acc is None else acc + part
    y_ref[...] = x1 + _rms(acc, g_fpost_ref[...])


def _proj_ffn_call(o, x, w_o, g_post, g_pre, w_gu, w_d, g_fpost):
    t = x.shape[0]
    tm = TOKEN_TILE
    gain = _const_spec((1, D_MODEL))
    return pl.pallas_call(
        _proj_ffn_kernel,
        grid=(t // tm,),
        in_specs=[
            pl.BlockSpec((tm, D_MODEL), lambda i: (i, 0)),
            pl.BlockSpec((tm, D_MODEL), lambda i: (i, 0)),
            _const_spec(w_o.shape),
            gain, gain,
            _const_spec(w_gu.shape),
            _const_spec(w_d.shape),
            gain,
        ],
        out_specs=pl.BlockSpec((tm, D_MODEL), lambda i: (i, 0)),
        out_shape=jax.ShapeDtypeStruct(x.shape, f32),
        compiler_params=pltpu.CompilerParams(
            dimension_semantics=("parallel",), vmem_limit_bytes=VMEM_LIMIT),
        name="proj_ffn",
    )(o, x, w_o, g_post, g_pre, w_gu, w_d, g_fpost)


def _rope_tables(seq):
    inv = 1.0 / (ROPE_THETA ** (jnp.arange(0, HEAD_DIM, 2, dtype=f32) / HEAD_DIM))
    ang = jnp.arange(seq, dtype=f32)[:, None] * inv[None, :]
    cos, sin = jnp.cos(ang), jnp.sin(ang)
    reps = LANES // HEAD_DIM
    return (jnp.tile(jnp.concatenate([cos, cos], axis=-1), (1, reps)),
            jnp.tile(jnp.concatenate([-sin, sin], axis=-1), (1, reps)))


def _trunk(x, p):
    batch, seq, _ = x.shape
    assert seq % TOKEN_TILE == 0 and seq % DIFF_KV_TILE == 0 and seq >= WIN_SPAN
    x = x.reshape(batch * seq, D_MODEL)
    cos, sin = _rope_tables(seq)
    for i in range(DEPTH):
        j = i // 2
        if i % 2 == 0:
            q, k, v = _qkv_call(x, p["norm_mix_pre"][i], p["a_w_in"][j], cos, sin, seq=seq,
                                q_dim=A_Q_DIM, k_dim=A_KV_DIM, v_dim=A_KV_DIM)
            o = _win_attn_call(q, k, v, p["a_sinks"][j], batch=batch, seq=seq)
            w_o = p["a_w_o"][j]
        else:
            lambda_init = 0.8 - 0.6 * math.exp(-0.3 * i)
            q, k, v = _qkv_call(x, p["norm_mix_pre"][i], p["b_w_in"][j], cos, sin, seq=seq,
                                q_dim=B_QK_DIM, k_dim=B_QK_DIM, v_dim=B_V_DIM)
            o = _diff_attn_call(q, k, v, p["b_lambda_q1"][j], p["b_lambda_k1"][j],
                                p["b_lambda_q2"][j], p["b_lambda_k2"][j], p["b_subln"][j],
                                batch=batch, seq=seq, lambda_init=lambda_init)
            w_o = p["b_w_o"][j]
        x = _proj_ffn_call(o, x, w_o, p["norm_mix_post"][i], p["norm_ffn_pre"][i],
                           p["ffn_w_gate_up"][i], p["ffn_w_down"][i], p["norm_ffn_post"][i])
    return x.reshape(batch, seq, D_MODEL)


def _prepare(norm_mix_pre, norm_mix_post, norm_ffn_pre, norm_ffn_post, a_w_in, a_w_o, a_sinks,
             b_w_in, b_w_o, b_lambda_q1, b_lambda_k1, b_lambda_q2, b_lambda_k2, b_subln,
             ffn_w_gate_up, ffn_w_down):
    row = lambda a: a[:, None, :]
    return {
        "norm_mix_pre": row(norm_mix_pre), "norm_mix_post": row(norm_mix_post),
        "norm_ffn_pre": row(norm_ffn_pre), "norm_ffn_post": row(norm_ffn_post),
        "a_w_in": a_w_in.astype(bf16), "a_w_o": a_w_o.astype(bf16), "a_sinks": a_sinks,
        "b_w_in": b_w_in.astype(bf16), "b_w_o": b_w_o.astype(bf16),
        "b_lambda_q1": row(b_lambda_q1), "b_lambda_k1": row(b_lambda_k1),
        "b_lambda_q2": row(b_lambda_q2), "b_lambda_k2": row(b_lambda_k2),
        "b_subln": row(b_subln),
        "ffn_w_gate_up": ffn_w_gate_up.astype(bf16), "ffn_w_down": ffn_w_down.astype(bf16),
    }


def kernel(x_prompt, x_sample, norm_mix_pre, norm_mix_post, norm_ffn_pre, norm_ffn_post, a_w_in, a_w_o, a_sinks, b_w_in, b_w_o, b_lambda_q1, b_lambda_k1, b_lambda_q2, b_lambda_k2, b_subln, ffn_w_gate_up, ffn_w_down):
    p = _prepare(norm_mix_pre, norm_mix_post, norm_ffn_pre, norm_ffn_post, a_w_in, a_w_o, a_sinks,
                 b_w_in, b_w_o, b_lambda_q1, b_lambda_k1, b_lambda_q2, b_lambda_k2, b_subln,
                 ffn_w_gate_up, ffn_w_down)
    return (_trunk(x_prompt, p), _trunk(x_sample, p))
```

```python
import functools
import math

import jax
import jax.numpy as jnp
from jax import lax
from jax.experimental import pallas as pl
from jax.experimental.pallas import tpu as pltpu

D_MODEL = 1024
DEPTH = 4
HEAD_DIM = 64
ROPE_THETA = 10000.0
RMS_EPS = 1e-6
WINDOW = 128
A_HEADS = 16
A_KV_HEADS = 4
A_GROUP = A_HEADS // A_KV_HEADS
A_Q_DIM = A_HEADS * HEAD_DIM
A_KV_DIM = A_KV_HEADS * HEAD_DIM
B_HEADS = 8
B_QK_DIM = B_HEADS * 2 * HEAD_DIM
B_V_DIM = B_HEADS * 2 * HEAD_DIM
D_FF = 2816

LANES = 128
VMEM_LIMIT = 56 * 1024 * 1024

TOKEN_TILE = 512
QKV_COL_CHUNK = 512
FFN_CHUNKS = ((0, 1024), (1024, 1024), (2048, 768))
WIN_Q_TILE = 256
WIN_SPAN = WIN_Q_TILE + 2 * WINDOW
DIFF_Q_TILE = 256
DIFF_KV_TILE = 512
ONES_ROWS = 16

_NT = (((1,), (1,)), ((), ()))

f32 = jnp.float32
bf16 = jnp.bfloat16


def _rms(x, g):
    return x * lax.rsqrt(jnp.mean(x * x, axis=-1, keepdims=True) + RMS_EPS) * g


def _const_spec(shape):
    return pl.BlockSpec(shape, lambda *_: (0,) * len(shape), pipeline_mode=pl.Buffered(1))


def _qkv_kernel(x_ref, g_ref, w_ref, cos_ref, sin_ref, q_ref, k_ref, v_ref, *, q_dim, k_dim):
    h = _rms(x_ref[...], g_ref[...]).astype(bf16)
    cos = cos_ref[...]
    sin = sin_ref[...]
    lane = lax.broadcasted_iota(jnp.int32, cos.shape, 1)
    first_half = (lane % HEAD_DIM) < (HEAD_DIM // 2)
    n_total = w_ref.shape[1]
    for c0 in range(0, n_total, QKV_COL_CHUNK):
        y = jnp.dot(h, w_ref[:, c0:c0 + QKV_COL_CHUNK], preferred_element_type=f32)
        for l0 in range(0, QKV_COL_CHUNK, LANES):
            col = c0 + l0
            yc = y[:, l0:l0 + LANES]
            if col < q_dim + k_dim:
                partner = jnp.where(first_half,
                                    pltpu.roll(yc, LANES - HEAD_DIM // 2, 1),
                                    pltpu.roll(yc, HEAD_DIM // 2, 1))
                yc = yc * cos + partner * sin
            if col < q_dim:
                q_ref[:, col:col + LANES] = (yc * (HEAD_DIM ** -0.5)).astype(bf16)
            elif col < q_dim + k_dim:
                k_ref[:, col - q_dim:col - q_dim + LANES] = yc.astype(bf16)
            else:
                v_ref[:, col - q_dim - k_dim:col - q_dim - k_dim + LANES] = yc.astype(bf16)


def _qkv_call(x, gain, w, cos, sin, *, seq, q_dim, k_dim, v_dim):
    t = x.shape[0]
    tm = TOKEN_TILE
    pos_blocks = seq // tm
    kern = functools.partial(_qkv_kernel, q_dim=q_dim, k_dim=k_dim)
    return pl.pallas_call(
        kern,
        grid=(t // tm,),
        in_specs=[
            pl.BlockSpec((tm, D_MODEL), lambda i: (i, 0)),
            _const_spec((1, D_MODEL)),
            _const_spec(w.shape),
            pl.BlockSpec((tm, LANES), lambda i: (i % pos_blocks, 0)),
            pl.BlockSpec((tm, LANES), lambda i: (i % pos_blocks, 0)),
        ],
        out_specs=[
            pl.BlockSpec((tm, q_dim), lambda i: (i, 0)),
            pl.BlockSpec((tm, k_dim), lambda i: (i, 0)),
            pl.BlockSpec((tm, v_dim), lambda i: (i, 0)),
        ],
        out_shape=[
            jax.ShapeDtypeStruct((t, q_dim), bf16),
            jax.ShapeDtypeStruct((t, k_dim), bf16),
            jax.ShapeDtypeStruct((t, v_dim), bf16),
        ],
        compiler_params=pltpu.CompilerParams(
            dimension_semantics=("parallel",), vmem_limit_bytes=VMEM_LIMIT),
        name="qkv_rope",
    )(x, gain, w, cos, sin)


def _win_head_order():
    order = []
    for pair in range(A_KV_HEADS // 2):
        for g in range(A_GROUP):
            order += [(2 * pair) * A_GROUP + g, (2 * pair + 1) * A_GROUP + g]
    return order


def _win_feature_perm():
    cols = []
    for h in _win_head_order():
        cols += range(h * HEAD_DIM, (h + 1) * HEAD_DIM)
    return jnp.asarray(cols, jnp.int32)


def _win_attn_kernel(sink_ref, q_ref, k_ref, v_ref, o_ref, vt_sc, *, seq):
    tq = WIN_Q_TILE
    i = pl.program_id(1)

    @pl.when(i == 0)
    def _():
        for pair in range(A_KV_HEADS // 2):
            for c0 in range(0, seq, WIN_SPAN):
                vt_sc[pair, 0:LANES, c0:c0 + WIN_SPAN] = (
                    v_ref[c0:c0 + WIN_SPAN, pair * LANES:(pair + 1) * LANES]
                    .astype(f32).T.astype(bf16))
            vt_sc[pair, LANES:LANES + ONES_ROWS, :] = jnp.ones((ONES_ROWS, seq), bf16)

    start = pl.multiple_of(jnp.clip(i * tq - WINDOW, 0, seq - WIN_SPAN), WINDOW)
    kpos = start + lax.broadcasted_iota(jnp.int32, (WIN_SPAN, 2 * tq), 0)
    lane = lax.broadcasted_iota(jnp.int32, (WIN_SPAN, 2 * tq), 1)
    qpos = i * tq + lane % tq
    valid = jnp.abs(qpos - kpos) <= WINDOW
    lane_row = lax.broadcasted_iota(jnp.int32, (1, 2 * tq), 1)
    feat = lax.broadcasted_iota(jnp.int32, (LANES, tq), 0)
    for pair in range(A_KV_HEADS // 2):
        kb = k_ref[pl.ds(start, WIN_SPAN), pair * LANES:(pair + 1) * LANES]
        vtb = vt_sc[pair, :, pl.ds(start, WIN_SPAN)]
        for g in range(A_GROUP):
            blk = pair * A_GROUP + g
            qt = q_ref[:, blk * LANES:(blk + 1) * LANES].astype(f32).T
            qst = jnp.concatenate([jnp.where(feat < HEAD_DIM, qt, 0.0),
                                   jnp.where(feat >= HEAD_DIM, qt, 0.0)], axis=1).astype(bf16)
            sink = jnp.where(lane_row < tq, sink_ref[(2 * pair) * A_GROUP + g],
                             sink_ref[(2 * pair + 1) * A_GROUP + g])
            st = jnp.dot(kb, qst, preferred_element_type=f32)
            st = jnp.where(valid, st, -jnp.inf)
            m = jnp.maximum(jnp.max(st, axis=0, keepdims=True), sink)
            pt = jnp.exp(st - m).astype(bf16)
            res = jnp.dot(vtb, pt, preferred_element_type=f32)
            ot = res[0:LANES, :] / (res[LANES:LANES + 1, :] + jnp.exp(sink - m))
            ob = jnp.where(feat < HEAD_DIM, ot[:, :tq], ot[:, tq:])
            o_ref[:, blk * LANES:(blk + 1) * LANES] = ob.T.astype(bf16)


def _win_attn_call(q, k, v, sinks, *, batch, seq):
    tq = WIN_Q_TILE
    n_blk = seq // tq
    kern = functools.partial(_win_attn_kernel, seq=seq)
    return pl.pallas_call(
        kern,
        grid=(batch, n_blk),
        in_specs=[
            pl.BlockSpec(memory_space=pltpu.SMEM),
            pl.BlockSpec((tq, A_Q_DIM), lambda b, i: (b * n_blk + i, 0)),
            pl.BlockSpec((seq, A_KV_DIM), lambda b, i: (b, 0)),
            pl.BlockSpec((seq, A_KV_DIM), lambda b, i: (b, 0)),
        ],
        out_specs=pl.BlockSpec((tq, A_Q_DIM), lambda b, i: (b * n_blk + i, 0)),
        out_shape=jax.ShapeDtypeStruct(q.shape, bf16),
        scratch_shapes=[pltpu.VMEM((A_KV_HEADS // 2, LANES + ONES_ROWS, seq), bf16)],
        compiler_params=pltpu.CompilerParams(
            dimension_semantics=("arbitrary", "arbitrary"), vmem_limit_bytes=VMEM_LIMIT),
        name="win_attn",
    )(sinks, q, k, v)


def _diff_attn_kernel(q_ref, k_ref, v_ref, lq1_ref, lk1_ref, lq2_ref, lk2_ref, g_ref, o_ref,
                      vt_sc, st_sc, acc_sc, *, seq, lambda_init):
    tq = DIFF_Q_TILE
    tk = DIFF_KV_TILE
    dv = 2 * HEAD_DIM

    @pl.when(pl.program_id(2) == 0)
    def _():
        for c0 in range(0, seq, tk):
            vt_sc[0:dv, c0:c0 + tk] = v_ref[c0:c0 + tk, :].astype(f32).T.astype(bf16)
        vt_sc[dv:dv + ONES_ROWS, :] = jnp.ones((ONES_ROWS, seq), bf16)

    qt = q_ref[...].astype(f32).T
    row = lax.broadcasted_iota(jnp.int32, qt.shape, 0)
    qst = jnp.concatenate([jnp.where(row < HEAD_DIM, qt, 0.0),
                           jnp.where(row >= HEAD_DIM, qt, 0.0)], axis=1).astype(bf16)
    acc_sc[...] = jnp.zeros(acc_sc.shape, f32)

    def scores(j, slot):
        off = pl.multiple_of(j * tk, tk)
        st = jnp.dot(k_ref[pl.ds(off, tk), :], qst, preferred_element_type=f32)
        st_sc[slot] = st
        return jnp.max(st, axis=0, keepdims=True)

    def accumulate(j, slot, m_prev, tile_max):
        off = pl.multiple_of(j * tk, tk)
        m_new = jnp.maximum(m_prev, tile_max)
        alpha = jnp.exp(m_prev - m_new)
        pt = jnp.exp(st_sc[slot] - m_new).astype(bf16)
        acc_sc[...] = alpha * acc_sc[...] + jnp.dot(vt_sc[:, pl.ds(off, tk)], pt,
                                                    preferred_element_type=f32)
        return m_new

    n_tiles = seq // tk
    m0 = jnp.full((1, 2 * tq), -jnp.inf, f32)
    max0 = scores(0, 0)

    def pair(jj, carry):
        m, max_even = carry
        j = 2 * jj
        max_odd = scores(j + 1, 1)
        m = accumulate(j, 0, m, max_even)
        max_even = scores(j + 2, 0)
        m = accumulate(j + 1, 1, m, max_odd)
        return m, max_even

    m, max_even = lax.fori_loop(0, n_tiles // 2 - 1, pair, (m0, max0))
    max_odd = scores(n_tiles - 1, 1)
    m = accumulate(n_tiles - 2, 0, m, max_even)
    accumulate(n_tiles - 1, 1, m, max_odd)

    lam = (jnp.exp(jnp.sum(lq1_ref[...] * lk1_ref[...], axis=-1, keepdims=True))
           - jnp.exp(jnp.sum(lq2_ref[...] * lk2_ref[...], axis=-1, keepdims=True))
           + lambda_init)
    acc = acc_sc[...]
    ot = acc[0:dv, :] / acc[dv:dv + 1, :]
    o = (ot[:, :tq] - lam * ot[:, tq:]).T
    o_ref[...] = (_rms(o, g_ref[...]) * (1.0 - lambda_init)).astype(bf16)


def _diff_attn_call(q, k, v, lq1, lk1, lq2, lk2, subln_g, *, batch, seq, lambda_init):
    tq = DIFF_Q_TILE
    n_q = seq // tq
    kern = functools.partial(_diff_attn_kernel, seq=seq, lambda_init=lambda_init)
    vec = _const_spec((1, HEAD_DIM))
    return pl.pallas_call(
        kern,
        grid=(batch, B_HEADS, n_q),
        in_specs=[
            pl.BlockSpec((tq, LANES), lambda b, h, i: (b * n_q + i, h)),
            pl.BlockSpec((seq, LANES), lambda b, h, i: (b, h)),
            pl.BlockSpec((seq, LANES), lambda b, h, i: (b, h)),
            vec, vec, vec, vec,
            _const_spec((1, 2 * HEAD_DIM)),
        ],
        out_specs=pl.BlockSpec((tq, LANES), lambda b, h, i: (b * n_q + i, h)),
        out_shape=jax.ShapeDtypeStruct(q.shape, bf16),
        scratch_shapes=[
            pltpu.VMEM((2 * HEAD_DIM + ONES_ROWS, seq), bf16),
            pltpu.VMEM((2, DIFF_KV_TILE, 2 * tq), f32),
            pltpu.VMEM((2 * HEAD_DIM + ONES_ROWS, 2 * tq), f32),
        ],
        compiler_params=pltpu.CompilerParams(
            dimension_semantics=("arbitrary", "arbitrary", "arbitrary"),
            vmem_limit_bytes=VMEM_LIMIT),
        name="diff_attn",
    )(q, k, v, lq1, lk1, lq2, lk2, subln_g)


def _proj_ffn_kernel(o_ref, x_ref, wo_ref, g_post_ref, g_pre_ref, wgu_ref, wd_ref, g_fpost_ref,
                     y_ref):
    mixed = jnp.dot(o_ref[...], wo_ref[...], preferred_element_type=f32)
    x1 = x_ref[...] + _rms(mixed, g_post_ref[...])
    h = _rms(x1, g_pre_ref[...]).astype(bf16)
    acc = None
    for c0, width in FFN_CHUNKS:
        gate = jnp.dot(h, wgu_ref[:, c0:c0 + width], preferred_element_type=f32)
        up = jnp.dot(h, wgu_ref[:, D_FF + c0:D_FF + c0 + width], preferred_element_type=f32)
        act = (gate * jax.nn.sigmoid(gate) * up).astype(bf16)
        part = jnp.dot(act, wd_ref[c0:c0 + width, :], preferred_element_type=f32)
        acc = part if acc is None else acc + part
    y_ref[...] = x1 + _rms(acc, g_fpost_ref[...])


def _proj_ffn_call(o, x, w_o, g_post, g_pre, w_gu, w_d, g_fpost):
    t = x.shape[0]
    tm = TOKEN_TILE
    gain = _const_spec((1, D_MODEL))
    return pl.pallas_call(
        _proj_ffn_kernel,
        grid=(t // tm,),
        in_specs=[
            pl.BlockSpec((tm, D_MODEL), lambda i: (i, 0)),
            pl.BlockSpec((tm, D_MODEL), lambda i: (i, 0)),
            _const_spec(w_o.shape),
            gain, gain,
            _const_spec(w_gu.shape),
            _const_spec(w_d.shape),
            gain,
        ],
        out_specs=pl.BlockSpec((tm, D_MODEL), lambda i: (i, 0)),
        out_shape=jax.ShapeDtypeStruct(x.shape, f32),
        compiler_params=pltpu.CompilerParams(
            dimension_semantics=("parallel",), vmem_limit_bytes=VMEM_LIMIT),
        name="proj_ffn",
    )(o, x, w_o, g_post, g_pre, w_gu, w_d, g_fpost)


def _rope_tables(seq):
    inv = 1.0 / (ROPE_THETA ** (jnp.arange(0, HEAD_DIM, 2, dtype=f32) / HEAD_DIM))
    ang = jnp.arange(seq, dtype=f32)[:, None] * inv[None, :]
    cos, sin = jnp.cos(ang), jnp.sin(ang)
    reps = LANES // HEAD_DIM
    return (jnp.tile(jnp.concatenate([cos, cos], axis=-1), (1, reps)),
            jnp.tile(jnp.concatenate([-sin, sin], axis=-1), (1, reps)))


def _trunk(x, p):
    batch, seq, _ = x.shape
    assert seq % TOKEN_TILE == 0 and seq % (2 * DIFF_KV_TILE) == 0 and seq % WIN_SPAN == 0
    x = x.reshape(batch * seq, D_MODEL)
    cos, sin = _rope_tables(seq)
    for i in range(DEPTH):
        j = i // 2
        if i % 2 == 0:
            q, k, v = _qkv_call(x, p["norm_mix_pre"][i], p["a_w_in"][j], cos, sin, seq=seq,
                                q_dim=A_Q_DIM, k_dim=A_KV_DIM, v_dim=A_KV_DIM)
            o = _win_attn_call(q, k, v, p["a_sinks"][j], batch=batch, seq=seq)
            w_o = p["a_w_o"][j]
        else:
            lambda_init = 0.8 - 0.6 * math.exp(-0.3 * i)
            q, k, v = _qkv_call(x, p["norm_mix_pre"][i], p["b_w_in"][j], cos, sin, seq=seq,
                                q_dim=B_QK_DIM, k_dim=B_QK_DIM, v_dim=B_V_DIM)
            o = _diff_attn_call(q, k, v, p["b_lambda_q1"][j], p["b_lambda_k1"][j],
                                p["b_lambda_q2"][j], p["b_lambda_k2"][j], p["b_subln"][j],
                                batch=batch, seq=seq, lambda_init=lambda_init)
            w_o = p["b_w_o"][j]
        x = _proj_ffn_call(o, x, w_o, p["norm_mix_post"][i], p["norm_ffn_pre"][i],
                           p["ffn_w_gate_up"][i], p["ffn_w_down"][i], p["norm_ffn_post"][i])
    return x.reshape(batch, seq, D_MODEL)


def _prepare(norm_mix_pre, norm_mix_post, norm_ffn_pre, norm_ffn_post, a_w_in, a_w_o, a_sinks,
             b_w_in, b_w_o, b_lambda_q1, b_lambda_k1, b_lambda_q2, b_lambda_k2, b_subln,
             ffn_w_gate_up, ffn_w_down):
    row = lambda a: a[:, None, :]
    perm = _win_feature_perm()
    return {
        "norm_mix_pre": row(norm_mix_pre), "norm_mix_post": row(norm_mix_post),
        "norm_ffn_pre": row(norm_ffn_pre), "norm_ffn_post": row(norm_ffn_post),
        "a_w_in": jnp.concatenate([a_w_in[:, :, :A_Q_DIM][:, :, perm], a_w_in[:, :, A_Q_DIM:]],
                                  axis=-1).astype(bf16),
        "a_w_o": a_w_o[:, perm, :].astype(bf16), "a_sinks": a_sinks,
        "b_w_in": b_w_in.astype(bf16), "b_w_o": b_w_o.astype(bf16),
        "b_lambda_q1": row(b_lambda_q1), "b_lambda_k1": row(b_lambda_k1),
        "b_lambda_q2": row(b_lambda_q2), "b_lambda_k2": row(b_lambda_k2),
        "b_subln": row(b_subln),
        "ffn_w_gate_up": ffn_w_gate_up.astype(bf16), "ffn_w_down": ffn_w_down.astype(bf16),
    }


def kernel(x_prompt, x_sample, norm_mix_pre, norm_mix_post, norm_ffn_pre, norm_ffn_post, a_w_in, a_w_o, a_sinks, b_w_in, b_w_o, b_lambda_q1, b_lambda_k1, b_lambda_q2, b_lambda_k2, b_subln, ffn_w_gate_up, ffn_w_down):
    p = _prepare(norm_mix_pre, norm_mix_post, norm_ffn_pre, norm_ffn_post, a_w_in, a_w_o, a_sinks,
                 b_w_in, b_w_o, b_lambda_q1, b_lambda_k1, b_lambda_q2, b_lambda_k2, b_subln,
                 ffn_w_gate_up, ffn_w_down)
    return (_trunk(x_prompt, p), _trunk(x_sample, p))
```

```python
import functools
import math

import jax
import jax.numpy as jnp
from jax import lax
from jax.experimental import pallas as pl
from jax.experimental.pallas import tpu as pltpu

D_MODEL = 1024
DEPTH = 4
HEAD_DIM = 64
ROPE_THETA = 10000.0
RMS_EPS = 1e-6
WINDOW = 128
A_HEADS = 16
A_KV_HEADS = 4
A_GROUP = A_HEADS // A_KV_HEADS
A_Q_DIM = A_HEADS * HEAD_DIM
A_KV_DIM = A_KV_HEADS * HEAD_DIM
B_HEADS = 8
B_QK_DIM = B_HEADS * 2 * HEAD_DIM
B_V_DIM = B_HEADS * 2 * HEAD_DIM
D_FF = 2816

LANES = 128
VMEM_LIMIT = 56 * 1024 * 1024

TOKEN_TILE = 512
QKV_COL_CHUNK = 512
FFN_CHUNKS = ((0, 1024), (1024, 1024), (2048, 768))
WIN_Q_TILE = 256
WIN_SPAN = WIN_Q_TILE + 2 * WINDOW
DIFF_Q_TILE = 512
DIFF_KV_TILE = 1024
ONES_ROWS = 16

_NT = (((1,), (1,)), ((), ()))

f32 = jnp.float32
bf16 = jnp.bfloat16


def _rms(x, g):
    return x * lax.rsqrt(jnp.mean(x * x, axis=-1, keepdims=True) + RMS_EPS) * g


def _const_spec(shape):
    return pl.BlockSpec(shape, lambda *_: (0,) * len(shape), pipeline_mode=pl.Buffered(1))


def _qkv_kernel(x_ref, g_ref, w_ref, cos_ref, sin_ref, q_ref, k_ref, v_ref, *, q_dim, k_dim):
    h = _rms(x_ref[...], g_ref[...]).astype(bf16)
    cos = cos_ref[...]
    sin = sin_ref[...]
    lane = lax.broadcasted_iota(jnp.int32, cos.shape, 1)
    first_half = (lane % HEAD_DIM) < (HEAD_DIM // 2)
    n_total = w_ref.shape[1]
    for c0 in range(0, n_total, QKV_COL_CHUNK):
        y = jnp.dot(h, w_ref[:, c0:c0 + QKV_COL_CHUNK], preferred_element_type=f32)
        for l0 in range(0, QKV_COL_CHUNK, LANES):
            col = c0 + l0
            yc = y[:, l0:l0 + LANES]
            if col < q_dim + k_dim:
                partner = jnp.where(first_half,
                                    pltpu.roll(yc, LANES - HEAD_DIM // 2, 1),
                                    pltpu.roll(yc, HEAD_DIM // 2, 1))
                yc = yc * cos + partner * sin
            if col < q_dim:
                q_ref[:, col:col + LANES] = (yc * (HEAD_DIM ** -0.5)).astype(bf16)
            elif col < q_dim + k_dim:
                k_ref[:, col - q_dim:col - q_dim + LANES] = yc.astype(bf16)
            else:
                v_ref[:, col - q_dim - k_dim:col - q_dim - k_dim + LANES] = yc.astype(bf16)


def _qkv_call(x, gain, w, cos, sin, *, seq, q_dim, k_dim, v_dim):
    t = x.shape[0]
    tm = TOKEN_TILE
    pos_blocks = seq // tm
    kern = functools.partial(_qkv_kernel, q_dim=q_dim, k_dim=k_dim)
    return pl.pallas_call(
        kern,
        grid=(t // tm,),
        in_specs=[
            pl.BlockSpec((tm, D_MODEL), lambda i: (i, 0)),
            _const_spec((1, D_MODEL)),
            _const_spec(w.shape),
            pl.BlockSpec((tm, LANES), lambda i: (i % pos_blocks, 0)),
            pl.BlockSpec((tm, LANES), lambda i: (i % pos_blocks, 0)),
        ],
        out_specs=[
            pl.BlockSpec((tm, q_dim), lambda i: (i, 0)),
            pl.BlockSpec((tm, k_dim), lambda i: (i, 0)),
            pl.BlockSpec((tm, v_dim), lambda i: (i, 0)),
        ],
        out_shape=[
            jax.ShapeDtypeStruct((t, q_dim), bf16),
            jax.ShapeDtypeStruct((t, k_dim), bf16),
            jax.ShapeDtypeStruct((t, v_dim), bf16),
        ],
        compiler_params=pltpu.CompilerParams(
            dimension_semantics=("parallel",), vmem_limit_bytes=VMEM_LIMIT),
        name="qkv_rope",
    )(x, gain, w, cos, sin)


def _win_head_order():
    order = []
    for pair in range(A_KV_HEADS // 2):
        for g in range(A_GROUP):
            order += [(2 * pair) * A_GROUP + g, (2 * pair + 1) * A_GROUP + g]
    return order


def _win_feature_perm():
    cols = []
    for h in _win_head_order():
        cols += range(h * HEAD_DIM, (h + 1) * HEAD_DIM)
    return jnp.asarray(cols, jnp.int32)


def _win_attn_kernel(sink_ref, q_ref, k_ref, v_ref, o_ref, vt_sc, *, seq):
    tq = WIN_Q_TILE
    i = pl.program_id(1)

    @pl.when(i == 0)
    def _():
        for pair in range(A_KV_HEADS // 2):
            for c0 in range(0, seq, WIN_SPAN):
                vt_sc[pair, 0:LANES, c0:c0 + WIN_SPAN] = (
                    v_ref[c0:c0 + WIN_SPAN, pair * LANES:(pair + 1) * LANES]
                    .astype(f32).T.astype(bf16))
            vt_sc[pair, LANES:LANES + ONES_ROWS, :] = jnp.ones((ONES_ROWS, seq), bf16)

    start = pl.multiple_of(jnp.clip(i * tq - WINDOW, 0, seq - WIN_SPAN), WINDOW)
    kpos = start + lax.broadcasted_iota(jnp.int32, (WIN_SPAN, 2 * tq), 0)
    lane = lax.broadcasted_iota(jnp.int32, (WIN_SPAN, 2 * tq), 1)
    qpos = i * tq + lane % tq
    valid = jnp.abs(qpos - kpos) <= WINDOW
    lane_row = lax.broadcasted_iota(jnp.int32, (1, 2 * tq), 1)
    feat = lax.broadcasted_iota(jnp.int32, (LANES, tq), 0)
    for pair in range(A_KV_HEADS // 2):
        kb = k_ref[pl.ds(start, WIN_SPAN), pair * LANES:(pair + 1) * LANES]
        vtb = vt_sc[pair, :, pl.ds(start, WIN_SPAN)]
        for g in range(A_GROUP):
            blk = pair * A_GROUP + g
            qt = q_ref[:, blk * LANES:(blk + 1) * LANES].astype(f32).T
            qst = jnp.concatenate([jnp.where(feat < HEAD_DIM, qt, 0.0),
                                   jnp.where(feat >= HEAD_DIM, qt, 0.0)], axis=1).astype(bf16)
            sink = jnp.where(lane_row < tq, sink_ref[(2 * pair) * A_GROUP + g],
                             sink_ref[(2 * pair + 1) * A_GROUP + g])
            st = jnp.dot(kb, qst, preferred_element_type=f32)
            st = jnp.where(valid, st, -jnp.inf)
            m = jnp.maximum(jnp.max(st, axis=0, keepdims=True), sink)
            pt = jnp.exp(st - m).astype(bf16)
            res = jnp.dot(vtb, pt, preferred_element_type=f32)
            ot = res[0:LANES, :] / (res[LANES:LANES + 1, :] + jnp.exp(sink - m))
            ob = jnp.where(feat < HEAD_DIM, ot[:, :tq], ot[:, tq:])
            o_ref[:, blk * LANES:(blk + 1) * LANES] = ob.T.astype(bf16)


def _win_attn_call(q, k, v, sinks, *, batch, seq):
    tq = WIN_Q_TILE
    n_blk = seq // tq
    kern = functools.partial(_win_attn_kernel, seq=seq)
    return pl.pallas_call(
        kern,
        grid=(batch, n_blk),
        in_specs=[
            pl.BlockSpec(memory_space=pltpu.SMEM),
            pl.BlockSpec((tq, A_Q_DIM), lambda b, i: (b * n_blk + i, 0)),
            pl.BlockSpec((seq, A_KV_DIM), lambda b, i: (b, 0)),
            pl.BlockSpec((seq, A_KV_DIM), lambda b, i: (b, 0)),
        ],
        out_specs=pl.BlockSpec((tq, A_Q_DIM), lambda b, i: (b * n_blk + i, 0)),
        out_shape=jax.ShapeDtypeStruct(q.shape, bf16),
        scratch_shapes=[pltpu.VMEM((A_KV_HEADS // 2, LANES + ONES_ROWS, seq), bf16)],
        compiler_params=pltpu.CompilerParams(
            dimension_semantics=("arbitrary", "arbitrary"), vmem_limit_bytes=VMEM_LIMIT),
        name="win_attn",
    )(sinks, q, k, v)


def _diff_attn_kernel(q_ref, k_ref, v_ref, lq1_ref, lk1_ref, lq2_ref, lk2_ref, g_ref, o_ref,
                      vt_sc, qst_sc, st_sc, acc_sc, *, seq, lambda_init):
    tq = DIFF_Q_TILE
    tk = DIFF_KV_TILE
    dv = 2 * HEAD_DIM
    n_q = seq // tq
    n_tiles = seq // tk

    for c0 in range(0, seq, tk):
        vt_sc[0:dv, c0:c0 + tk] = v_ref[c0:c0 + tk, :].astype(f32).T.astype(bf16)
    vt_sc[dv:dv + ONES_ROWS, :] = jnp.ones((ONES_ROWS, seq), bf16)

    lam = (jnp.exp(jnp.sum(lq1_ref[...] * lk1_ref[...], axis=-1, keepdims=True))
           - jnp.exp(jnp.sum(lq2_ref[...] * lk2_ref[...], axis=-1, keepdims=True))
           + lambda_init)

    def load_queries(i):
        off = pl.multiple_of(i * tq, tq)
        qt = q_ref[pl.ds(off, tq), :].astype(f32).T
        row = lax.broadcasted_iota(jnp.int32, qt.shape, 0)
        qst_sc[...] = jnp.concatenate([jnp.where(row < HEAD_DIM, qt, 0.0),
                                       jnp.where(row >= HEAD_DIM, qt, 0.0)],
                                      axis=1).astype(bf16)

    def scores(j, slot):
        off = pl.multiple_of(j * tk, tk)
        st = jnp.dot(k_ref[pl.ds(off, tk), :], qst_sc[...],
                     preferred_element_type=f32)
        st_sc[slot] = st
        return jnp.max(st, axis=0, keepdims=True)

    def accumulate(j, slot, m_prev, tile_max):
        off = pl.multiple_of(j * tk, tk)
        m_new = jnp.maximum(m_prev, tile_max)
        alpha = jnp.exp(m_prev - m_new)
        pt = jnp.exp(st_sc[slot] - m_new).astype(bf16)
        acc_sc[...] = alpha * acc_sc[...] + jnp.dot(vt_sc[:, pl.ds(off, tk)], pt,
                                                    preferred_element_type=f32)
        return m_new

    def pair(jj, carry):
        m, max_even = carry
        j = 2 * jj
        max_odd = scores(j + 1, 1)
        m = accumulate(j, 0, m, max_even)
        max_even = scores(j + 2, 0)
        m = accumulate(j + 1, 1, m, max_odd)
        return m, max_even

    def query_tile(i, max_even):
        acc_sc[...] = jnp.zeros(acc_sc.shape, f32)
        m = jnp.full((1, 2 * tq), -jnp.inf, f32)
        m, max_even = lax.fori_loop(0, n_tiles // 2 - 1, pair, (m, max_even))
        max_odd = scores(n_tiles - 1, 1)
        m = accumulate(n_tiles - 2, 0, m, max_even)
        load_queries(jnp.minimum(i + 1, n_q - 1))
        max_next = scores(0, 0)
        accumulate(n_tiles - 1, 1, m, max_odd)
        acc = acc_sc[...]
        ot = acc[0:dv, :] / acc[dv:dv + 1, :]
        o = (ot[:, :tq] - lam * ot[:, tq:]).T
        off = pl.multiple_of(i * tq, tq)
        o_ref[pl.ds(off, tq), :] = (_rms(o, g_ref[...]) * (1.0 - lambda_init)).astype(bf16)
        return max_next

    load_queries(0)
    lax.fori_loop(0, n_q, query_tile, scores(0, 0))


def _diff_attn_call(q, k, v, lq1, lk1, lq2, lk2, subln_g, *, batch, seq, lambda_init):
    tq = DIFF_Q_TILE
    kern = functools.partial(_diff_attn_kernel, seq=seq, lambda_init=lambda_init)
    vec = _const_spec((1, HEAD_DIM))
    head = pl.BlockSpec((seq, LANES), lambda b, h: (b, h))
    return pl.pallas_call(
        kern,
        grid=(batch, B_HEADS),
        in_specs=[head, head, head, vec, vec, vec, vec, _const_spec((1, 2 * HEAD_DIM))],
        out_specs=head,
        out_shape=jax.ShapeDtypeStruct(q.shape, bf16),
        scratch_shapes=[
            pltpu.VMEM((2 * HEAD_DIM + ONES_ROWS, seq), bf16),
            pltpu.VMEM((2 * HEAD_DIM, 2 * tq), bf16),
            pltpu.VMEM((2, DIFF_KV_TILE, 2 * tq), f32),
            pltpu.VMEM((2 * HEAD_DIM + ONES_ROWS, 2 * tq), f32),
        ],
        compiler_params=pltpu.CompilerParams(
            dimension_semantics=("parallel", "parallel"), vmem_limit_bytes=VMEM_LIMIT),
        name="diff_attn",
    )(q, k, v, lq1, lk1, lq2, lk2, subln_g)


def _proj_ffn_kernel(o_ref, x_ref, wo_ref, g_post_ref, g_pre_ref, wgu_ref, wd_ref, g_fpost_ref,
                     y_ref):
    mixed = jnp.dot(o_ref[...], wo_ref[...], preferred_element_type=f32)
    x1 = x_ref[...] + _rms(mixed, g_post_ref[...])
    h = _rms(x1, g_pre_ref[...]).astype(bf16)
    acc = None
    for c0, width in FFN_CHUNKS:
        gate = jnp.dot(h, wgu_ref[:, c0:c0 + width], preferred_element_type=f32)
        up = jnp.dot(h, wgu_ref[:, D_FF + c0:D_FF + c0 + width], preferred_element_type=f32)
        act = (gate * jax.nn.sigmoid(gate) * up).astype(bf16)
        part = jnp.dot(act, wd_ref[c0:c0 + width, :], preferred_element_type=f32)
        acc = part if acc is None else acc + part
    y_ref[...] = x1 + _rms(acc, g_fpost_ref[...])


def _proj_ffn_call(o, x, w_o, g_post, g_pre, w_gu, w_d, g_fpost):
    t = x.shape[0]
    tm = TOKEN_TILE
    gain = _const_spec((1, D_MODEL))
    return pl.pallas_call(
        _proj_ffn_kernel,
        grid=(t // tm,),
        in_specs=[
            pl.BlockSpec((tm, D_MODEL), lambda i: (i, 0)),
            pl.BlockSpec((tm, D_MODEL), lambda i: (i, 0)),
            _const_spec(w_o.shape),
            gain, gain,
            _const_spec(w_gu.shape),
            _const_spec(w_d.shape),
            gain,
        ],
        out_specs=pl.BlockSpec((tm, D_MODEL), lambda i: (i, 0)),
        out_shape=jax.ShapeDtypeStruct(x.shape, f32),
        compiler_params=pltpu.CompilerParams(
            dimension_semantics=("parallel",), vmem_limit_bytes=VMEM_LIMIT),
        name="proj_ffn",
    )(o, x, w_o, g_post, g_pre, w_gu, w_d, g_fpost)


def _rope_tables(seq):
    inv = 1.0 / (ROPE_THETA ** (jnp.arange(0, HEAD_DIM, 2, dtype=f32) / HEAD_DIM))
    ang = jnp.arange(seq, dtype=f32)[:, None] * inv[None, :]
    cos, sin = jnp.cos(ang), jnp.sin(ang)
    reps = LANES // HEAD_DIM
    return (jnp.tile(jnp.concatenate([cos, cos], axis=-1), (1, reps)),
            jnp.tile(jnp.concatenate([-sin, sin], axis=-1), (1, reps)))


def _trunk(x, p):
    batch, seq, _ = x.shape
    assert seq % TOKEN_TILE == 0 and seq % (2 * DIFF_KV_TILE) == 0 and seq % WIN_SPAN == 0
    x = x.reshape(batch * seq, D_MODEL)
    cos, sin = _rope_tables(seq)
    for i in range(DEPTH):
        j = i // 2
        if i % 2 == 0:
            q, k, v = _qkv_call(x, p["norm_mix_pre"][i], p["a_w_in"][j], cos, sin, seq=seq,
                                q_dim=A_Q_DIM, k_dim=A_KV_DIM, v_dim=A_KV_DIM)
            o = _win_attn_call(q, k, v, p["a_sinks"][j], batch=batch, seq=seq)
            w_o = p["a_w_o"][j]
        else:
            lambda_init = 0.8 - 0.6 * math.exp(-0.3 * i)
            q, k, v = _qkv_call(x, p["norm_mix_pre"][i], p["b_w_in"][j], cos, sin, seq=seq,
                                q_dim=B_QK_DIM, k_dim=B_QK_DIM, v_dim=B_V_DIM)
            o = _diff_attn_call(q, k, v, p["b_lambda_q1"][j], p["b_lambda_k1"][j],
                                p["b_lambda_q2"][j], p["b_lambda_k2"][j], p["b_subln"][j],
                                batch=batch, seq=seq, lambda_init=lambda_init)
            w_o = p["b_w_o"][j]
        x = _proj_ffn_call(o, x, w_o, p["norm_mix_post"][i], p["norm_ffn_pre"][i],
                           p["ffn_w_gate_up"][i], p["ffn_w_down"][i], p["norm_ffn_post"][i])
    return x.reshape(batch, seq, D_MODEL)


def _prepare(norm_mix_pre, norm_mix_post, norm_ffn_pre, norm_ffn_post, a_w_in, a_w_o, a_sinks,
             b_w_in, b_w_o, b_lambda_q1, b_lambda_k1, b_lambda_q2, b_lambda_k2, b_subln,
             ffn_w_gate_up, ffn_w_down):
    row = lambda a: a[:, None, :]
    perm = _win_feature_perm()
    return {
        "norm_mix_pre": row(norm_mix_pre), "norm_mix_post": row(norm_mix_post),
        "norm_ffn_pre": row(norm_ffn_pre), "norm_ffn_post": row(norm_ffn_post),
        "a_w_in": jnp.concatenate([a_w_in[:, :, :A_Q_DIM][:, :, perm], a_w_in[:, :, A_Q_DIM:]],
                                  axis=-1).astype(bf16),
        "a_w_o": a_w_o[:, perm, :].astype(bf16), "a_sinks": a_sinks,
        "b_w_in": b_w_in.astype(bf16), "b_w_o": b_w_o.astype(bf16),
        "b_lambda_q1": row(b_lambda_q1), "b_lambda_k1": row(b_lambda_k1),
        "b_lambda_q2": row(b_lambda_q2), "b_lambda_k2": row(b_lambda_k2),
        "b_subln": row(b_subln),
        "ffn_w_gate_up": ffn_w_gate_up.astype(bf16), "ffn_w_down": ffn_w_down.astype(bf16),
    }


def kernel(x_prompt, x_sample, norm_mix_pre, norm_mix_post, norm_ffn_pre, norm_ffn_post, a_w_in, a_w_o, a_sinks, b_w_in, b_w_o, b_lambda_q1, b_lambda_k1, b_lambda_q2, b_lambda_k2, b_subln, ffn_w_gate_up, ffn_w_down):
    p = _prepare(norm_mix_pre, norm_mix_post, norm_ffn_pre, norm_ffn_post, a_w_in, a_w_o, a_sinks,
                 b_w_in, b_w_o, b_lambda_q1, b_lambda_k1, b_lambda_q2, b_lambda_k2, b_subln,
                 ffn_w_gate_up, ffn_w_down)
    return (_trunk(x_prompt, p), _trunk(x_sample, p))
```

```python
import functools
import math

import jax
import jax.numpy as jnp
from jax import lax
from jax.experimental import pallas as pl
from jax.experimental.pallas import tpu as pltpu

D_MODEL = 1024
DEPTH = 4
HEAD_DIM = 64
ROPE_THETA = 10000.0
RMS_EPS = 1e-6
WINDOW = 128
A_HEADS = 16
A_KV_HEADS = 4
A_GROUP = A_HEADS // A_KV_HEADS
A_Q_DIM = A_HEADS * HEAD_DIM
A_KV_DIM = A_KV_HEADS * HEAD_DIM
B_HEADS = 8
B_QK_DIM = B_HEADS * 2 * HEAD_DIM
B_V_DIM = B_HEADS * 2 * HEAD_DIM
D_FF = 2816

LANES = 128
VMEM_LIMIT = 56 * 1024 * 1024

TOKEN_TILE = 512
QKV_COL_CHUNK = 512
FFN_CHUNKS = ((0, 1024), (1024, 1024), (2048, 768))
WIN_Q_TILE = 256
WIN_SPAN = WIN_Q_TILE + 2 * WINDOW
DIFF_Q_TILE = 512
DIFF_KV_TILE = 1024
ONES_ROWS = 16

LOG2E = math.log2(math.e)
Q_SCALE = HEAD_DIM ** -0.5 * LOG2E

f32 = jnp.float32
bf16 = jnp.bfloat16


def _rms(x, g):
    return x * lax.rsqrt(jnp.mean(x * x, axis=-1, keepdims=True) + RMS_EPS) * g


def _const_spec(shape):
    return pl.BlockSpec(shape, lambda *_: (0,) * len(shape), pipeline_mode=pl.Buffered(1))


def _qkv_kernel(x_ref, g_ref, w_ref, cos_ref, sin_ref, q_ref, k_ref, v_ref, *, q_dim, k_dim):
    h = _rms(x_ref[...], g_ref[...]).astype(bf16)
    cos = cos_ref[...]
    sin = sin_ref[...]
    lane = lax.broadcasted_iota(jnp.int32, cos.shape, 1)
    first_half = (lane % HEAD_DIM) < (HEAD_DIM // 2)
    n_total = w_ref.shape[1]
    for c0 in range(0, n_total, QKV_COL_CHUNK):
        y = jnp.dot(h, w_ref[:, c0:c0 + QKV_COL_CHUNK], preferred_element_type=f32)
        for l0 in range(0, QKV_COL_CHUNK, LANES):
            col = c0 + l0
            yc = y[:, l0:l0 + LANES]
            if col < q_dim + k_dim:
                partner = jnp.where(first_half,
                                    pltpu.roll(yc, LANES - HEAD_DIM // 2, 1),
                                    pltpu.roll(yc, HEAD_DIM // 2, 1))
                yc = yc * cos + partner * sin
            if col < q_dim:
                q_ref[:, col:col + LANES] = (yc * Q_SCALE).astype(bf16)
            elif col < q_dim + k_dim:
                k_ref[:, col - q_dim:col - q_dim + LANES] = yc.astype(bf16)
            else:
                v_ref[:, col - q_dim - k_dim:col - q_dim - k_dim + LANES] = yc.astype(bf16)


def _qkv_call(x, gain, w, cos, sin, *, seq, q_dim, k_dim, v_dim):
    t = x.shape[0]
    tm = TOKEN_TILE
    pos_blocks = seq // tm
    kern = functools.partial(_qkv_kernel, q_dim=q_dim, k_dim=k_dim)
    return pl.pallas_call(
        kern,
        grid=(t // tm,),
        in_specs=[
            pl.BlockSpec((tm, D_MODEL), lambda i: (i, 0)),
            _const_spec((1, D_MODEL)),
            _const_spec(w.shape),
            pl.BlockSpec((tm, LANES), lambda i: (i % pos_blocks, 0)),
            pl.BlockSpec((tm, LANES), lambda i: (i % pos_blocks, 0)),
        ],
        out_specs=[
            pl.BlockSpec((tm, q_dim), lambda i: (i, 0)),
            pl.BlockSpec((tm, k_dim), lambda i: (i, 0)),
            pl.BlockSpec((tm, v_dim), lambda i: (i, 0)),
        ],
        out_shape=[
            jax.ShapeDtypeStruct((t, q_dim), bf16),
            jax.ShapeDtypeStruct((t, k_dim), bf16),
            jax.ShapeDtypeStruct((t, v_dim), bf16),
        ],
        compiler_params=pltpu.CompilerParams(
            dimension_semantics=("parallel",), vmem_limit_bytes=VMEM_LIMIT),
        name="qkv_rope",
    )(x, gain, w, cos, sin)


def _win_head_order():
    order = []
    for pair in range(A_KV_HEADS // 2):
        for g in range(A_GROUP):
            order += [(2 * pair) * A_GROUP + g, (2 * pair + 1) * A_GROUP + g]
    return order


def _win_feature_perm():
    cols = []
    for h in _win_head_order():
        cols += range(h * HEAD_DIM, (h + 1) * HEAD_DIM)
    return jnp.asarray(cols, jnp.int32)


def _win_attn_kernel(sink_ref, q_ref, k_ref, v_ref, o_ref, vt_sc, bias_sc, st_sc, *, seq):
    tq = WIN_Q_TILE
    i = pl.program_id(1)

    @pl.when(i == 0)
    def _():
        for pair in range(A_KV_HEADS // 2):
            for c0 in range(0, seq, WIN_SPAN):
                vt_sc[pair, 0:LANES, c0:c0 + WIN_SPAN] = (
                    v_ref[c0:c0 + WIN_SPAN, pair * LANES:(pair + 1) * LANES]
                    .astype(f32).T.astype(bf16))
            vt_sc[pair, LANES:LANES + ONES_ROWS, :] = jnp.ones((ONES_ROWS, seq), bf16)

    start = pl.multiple_of(jnp.clip(i * tq - WINDOW, 0, seq - WIN_SPAN), WINDOW)
    kpos = start + lax.broadcasted_iota(jnp.int32, (WIN_SPAN, 2 * tq), 0)
    lane = lax.broadcasted_iota(jnp.int32, (WIN_SPAN, 2 * tq), 1)
    qpos = i * tq + lane % tq
    bias_sc[...] = jnp.where(jnp.abs(qpos - kpos) <= WINDOW, 0.0, -jnp.inf)
    lane_row = lax.broadcasted_iota(jnp.int32, (1, 2 * tq), 1)
    feat = lax.broadcasted_iota(jnp.int32, (LANES, tq), 0)
    n_blocks = (A_KV_HEADS // 2) * A_GROUP

    def scores(blk):
        pair = blk // A_GROUP
        kb = k_ref[pl.ds(start, WIN_SPAN), pair * LANES:(pair + 1) * LANES]
        qt = q_ref[:, blk * LANES:(blk + 1) * LANES].astype(f32).T
        qst = jnp.concatenate([jnp.where(feat < HEAD_DIM, qt, 0.0),
                               jnp.where(feat >= HEAD_DIM, qt, 0.0)], axis=1).astype(bf16)
        st = jnp.dot(kb, qst, preferred_element_type=f32) + bias_sc[...]
        st_sc[blk % 2] = st
        return jnp.max(st, axis=0, keepdims=True)

    def finish(blk, tile_max):
        pair, g = divmod(blk, A_GROUP)
        sink = LOG2E * jnp.where(lane_row < tq, sink_ref[(2 * pair) * A_GROUP + g],
                                 sink_ref[(2 * pair + 1) * A_GROUP + g])
        m = jnp.maximum(tile_max, sink)
        pt = jnp.exp2(st_sc[blk % 2] - m).astype(bf16)
        res = jnp.dot(vt_sc[pair, :, pl.ds(start, WIN_SPAN)], pt,
                      preferred_element_type=f32)
        ot = res[0:LANES, :] / (res[LANES:LANES + 1, :] + jnp.exp2(sink - m))
        ob = jnp.where(feat < HEAD_DIM, ot[:, :tq], ot[:, tq:])
        o_ref[:, blk * LANES:(blk + 1) * LANES] = ob.T.astype(bf16)

    tile_max = scores(0)
    for blk in range(n_blocks):
        next_max = scores(blk + 1) if blk + 1 < n_blocks else None
        finish(blk, tile_max)
        tile_max = next_max


def _win_attn_call(q, k, v, sinks, *, batch, seq):
    tq = WIN_Q_TILE
    n_blk = seq // tq
    kern = functools.partial(_win_attn_kernel, seq=seq)
    return pl.pallas_call(
        kern,
        grid=(batch, n_blk),
        in_specs=[
            pl.BlockSpec(memory_space=pltpu.SMEM),
            pl.BlockSpec((tq, A_Q_DIM), lambda b, i: (b * n_blk + i, 0)),
            pl.BlockSpec((seq, A_KV_DIM), lambda b, i: (b, 0)),
            pl.BlockSpec((seq, A_KV_DIM), lambda b, i: (b, 0)),
        ],
        out_specs=pl.BlockSpec((tq, A_Q_DIM), lambda b, i: (b * n_blk + i, 0)),
        out_shape=jax.ShapeDtypeStruct(q.shape, bf16),
        scratch_shapes=[
            pltpu.VMEM((A_KV_HEADS // 2, LANES + ONES_ROWS, seq), bf16),
            pltpu.VMEM((WIN_SPAN, 2 * tq), f32),
            pltpu.VMEM((2, WIN_SPAN, 2 * tq), f32),
        ],
        compiler_params=pltpu.CompilerParams(
            dimension_semantics=("arbitrary", "arbitrary"), vmem_limit_bytes=VMEM_LIMIT),
        name="win_attn",
    )(sinks, q, k, v)


def _diff_attn_kernel(q_ref, k_ref, v_ref, lq1_ref, lk1_ref, lq2_ref, lk2_ref, g_ref, o_ref,
                      vt_sc, qst_sc, st_sc, acc_sc, *, seq, lambda_init):
    tq = DIFF_Q_TILE
    tk = DIFF_KV_TILE
    dv = 2 * HEAD_DIM
    n_q = seq // tq
    n_tiles = seq // tk

    for c0 in range(0, seq, tk):
        vt_sc[0:dv, c0:c0 + tk] = v_ref[c0:c0 + tk, :].astype(f32).T.astype(bf16)
    vt_sc[dv:dv + ONES_ROWS, :] = jnp.ones((ONES_ROWS, seq), bf16)

    lam = (jnp.exp(jnp.sum(lq1_ref[...] * lk1_ref[...], axis=-1, keepdims=True))
           - jnp.exp(jnp.sum(lq2_ref[...] * lk2_ref[...], axis=-1, keepdims=True))
           + lambda_init)

    def load_queries(i):
        off = pl.multiple_of(i * tq, tq)
        qt = q_ref[pl.ds(off, tq), :].astype(f32).T
        row = lax.broadcasted_iota(jnp.int32, qt.shape, 0)
        qst_sc[...] = jnp.concatenate([jnp.where(row < HEAD_DIM, qt, 0.0),
                                       jnp.where(row >= HEAD_DIM, qt, 0.0)],
                                      axis=1).astype(bf16)

    def scores(j, slot):
        off = pl.multiple_of(j * tk, tk)
        st = jnp.dot(k_ref[pl.ds(off, tk), :], qst_sc[...],
                     preferred_element_type=f32)
        st_sc[slot] = st
        return jnp.max(st, axis=0, keepdims=True)

    def accumulate(j, slot, m_prev, tile_max):
        off = pl.multiple_of(j * tk, tk)
        m_new = jnp.maximum(m_prev, tile_max)
        alpha = jnp.exp2(m_prev - m_new)
        pt = jnp.exp2(st_sc[slot] - m_new).astype(bf16)
        acc_sc[...] = alpha * acc_sc[...] + jnp.dot(vt_sc[:, pl.ds(off, tk)], pt,
                                                    preferred_element_type=f32)
        return m_new

    def pair(jj, carry):
        m, max_even = carry
        j = 2 * jj
        max_odd = scores(j + 1, 1)
        m = accumulate(j, 0, m, max_even)
        max_even = scores(j + 2, 0)
        m = accumulate(j + 1, 1, m, max_odd)
        return m, max_even

    def query_tile(i, max_even):
        acc_sc[...] = jnp.zeros(acc_sc.shape, f32)
        m = jnp.full((1, 2 * tq), -jnp.inf, f32)
        m, max_even = lax.fori_loop(0, n_tiles // 2 - 1, pair, (m, max_even))
        max_odd = scores(n_tiles - 1, 1)
        m = accumulate(n_tiles - 2, 0, m, max_even)
        load_queries(jnp.minimum(i + 1, n_q - 1))
        max_next = scores(0, 0)
        accumulate(n_tiles - 1, 1, m, max_odd)
        acc = acc_sc[...]
        ot = acc[0:dv, :] / acc[dv:dv + 1, :]
        o = (ot[:, :tq] - lam * ot[:, tq:]).T
        off = pl.multiple_of(i * tq, tq)
        o_ref[pl.ds(off, tq), :] = (_rms(o, g_ref[...]) * (1.0 - lambda_init)).astype(bf16)
        return max_next

    load_queries(0)
    lax.fori_loop(0, n_q, query_tile, scores(0, 0))


def _diff_attn_call(q, k, v, lq1, lk1, lq2, lk2, subln_g, *, batch, seq, lambda_init):
    tq = DIFF_Q_TILE
    kern = functools.partial(_diff_attn_kernel, seq=seq, lambda_init=lambda_init)
    vec = _const_spec((1, HEAD_DIM))
    head = pl.BlockSpec((seq, LANES), lambda b, h: (b, h))
    return pl.pallas_call(
        kern,
        grid=(batch, B_HEADS),
        in_specs=[head, head, head, vec, vec, vec, vec, _const_spec((1, 2 * HEAD_DIM))],
        out_specs=head,
        out_shape=jax.ShapeDtypeStruct(q.shape, bf16),
        scratch_shapes=[
            pltpu.VMEM((2 * HEAD_DIM + ONES_ROWS, seq), bf16),
            pltpu.VMEM((2 * HEAD_DIM, 2 * tq), bf16),
            pltpu.VMEM((2, DIFF_KV_TILE, 2 * tq), f32),
            pltpu.VMEM((2 * HEAD_DIM + ONES_ROWS, 2 * tq), f32),
        ],
        compiler_params=pltpu.CompilerParams(
            dimension_semantics=("parallel", "parallel"), vmem_limit_bytes=VMEM_LIMIT),
        name="diff_attn",
    )(q, k, v, lq1, lk1, lq2, lk2, subln_g)


def _proj_ffn_kernel(o_ref, x_ref, wo_ref, g_post_ref, g_pre_ref, wgu_ref, wd_ref, g_fpost_ref,
                     y_ref):
    mixed = jnp.dot(o_ref[...], wo_ref[...], preferred_element_type=f32)
    x1 = x_ref[...] + _rms(mixed, g_post_ref[...])
    h = _rms(x1, g_pre_ref[...]).astype(bf16)
    acc = None
    for c0, width in FFN_CHUNKS:
        gate = jnp.dot(h, wgu_ref[:, c0:c0 + width], preferred_element_type=f32)
        up = jnp.dot(h, wgu_ref[:, D_FF + c0:D_FF + c0 + width], preferred_element_type=f32)
        act = (gate * jax.nn.sigmoid(gate) * up).astype(bf16)
        part = jnp.dot(act, wd_ref[c0:c0 + width, :], preferred_element_type=f32)
        acc = part if acc is None else acc + part
    y_ref[...] = x1 + _rms(acc, g_fpost_ref[...])


def _proj_ffn_call(o, x, w_o, g_post, g_pre, w_gu, w_d, g_fpost):
    t = x.shape[0]
    tm = TOKEN_TILE
    gain = _const_spec((1, D_MODEL))
    return pl.pallas_call(
        _proj_ffn_kernel,
        grid=(t // tm,),
        in_specs=[
            pl.BlockSpec((tm, D_MODEL), lambda i: (i, 0)),
            pl.BlockSpec((tm, D_MODEL), lambda i: (i, 0)),
            _const_spec(w_o.shape),
            gain, gain,
            _const_spec(w_gu.shape),
            _const_spec(w_d.shape),
            gain,
        ],
        out_specs=pl.BlockSpec((tm, D_MODEL), lambda i: (i, 0)),
        out_shape=jax.ShapeDtypeStruct(x.shape, f32),
        compiler_params=pltpu.CompilerParams(
            dimension_semantics=("parallel",), vmem_limit_bytes=VMEM_LIMIT),
        name="proj_ffn",
    )(o, x, w_o, g_post, g_pre, w_gu, w_d, g_fpost)


def _rope_tables(seq):
    inv = 1.0 / (ROPE_THETA ** (jnp.arange(0, HEAD_DIM, 2, dtype=f32) / HEAD_DIM))
    ang = jnp.arange(seq, dtype=f32)[:, None] * inv[None, :]
    cos, sin = jnp.cos(ang), jnp.sin(ang)
    reps = LANES // HEAD_DIM
    return (jnp.tile(jnp.concatenate([cos, cos], axis=-1), (1, reps)),
            jnp.tile(jnp.concatenate([-sin, sin], axis=-1), (1, reps)))


def _trunk(x, p):
    batch, seq, _ = x.shape
    assert seq % TOKEN_TILE == 0 and seq % (2 * DIFF_KV_TILE) == 0 and seq % WIN_SPAN == 0
    x = x.reshape(batch * seq, D_MODEL)
    cos, sin = _rope_tables(seq)
    for i in range(DEPTH):
        j = i // 2
        if i % 2 == 0:
            q, k, v = _qkv_call(x, p["norm_mix_pre"][i], p["a_w_in"][j], cos, sin, seq=seq,
                                q_dim=A_Q_DIM, k_dim=A_KV_DIM, v_dim=A_KV_DIM)
            o = _win_attn_call(q, k, v, p["a_sinks"][j], batch=batch, seq=seq)
            w_o = p["a_w_o"][j]
        else:
            lambda_init = 0.8 - 0.6 * math.exp(-0.3 * i)
            q, k, v = _qkv_call(x, p["norm_mix_pre"][i], p["b_w_in"][j], cos, sin, seq=seq,
                                q_dim=B_QK_DIM, k_dim=B_QK_DIM, v_dim=B_V_DIM)
            o = _diff_attn_call(q, k, v, p["b_lambda_q1"][j], p["b_lambda_k1"][j],
                                p["b_lambda_q2"][j], p["b_lambda_k2"][j], p["b_subln"][j],
                                batch=batch, seq=seq, lambda_init=lambda_init)
            w_o = p["b_w_o"][j]
        x = _proj_ffn_call(o, x, w_o, p["norm_mix_post"][i], p["norm_ffn_pre"][i],
                           p["ffn_w_gate_up"][i], p["ffn_w_down"][i], p["norm_ffn_post"][i])
    return x.reshape(batch, seq, D_MODEL)


def _prepare(norm_mix_pre, norm_mix_post, norm_ffn_pre, norm_ffn_post, a_w_in, a_w_o, a_sinks,
             b_w_in, b_w_o, b_lambda_q1, b_lambda_k1, b_lambda_q2, b_lambda_k2, b_subln,
             ffn_w_gate_up, ffn_w_down):
    row = lambda a: a[:, None, :]
    perm = _win_feature_perm()
    return {
        "norm_mix_pre": row(norm_mix_pre), "norm_mix_post": row(norm_mix_post),
        "norm_ffn_pre": row(norm_ffn_pre), "norm_ffn_post": row(norm_ffn_post),
        "a_w_in": jnp.concatenate([a_w_in[:, :, :A_Q_DIM][:, :, perm], a_w_in[:, :, A_Q_DIM:]],
                                  axis=-1).astype(bf16),
        "a_w_o": a_w_o[:, perm, :].astype(bf16), "a_sinks": a_sinks,
        "b_w_in": b_w_in.astype(bf16), "b_w_o": b_w_o.astype(bf16),
        "b_lambda_q1": row(b_lambda_q1), "b_lambda_k1": row(b_lambda_k1),
        "b_lambda_q2": row(b_lambda_q2), "b_lambda_k2": row(b_lambda_k2),
        "b_subln": row(b_subln),
        "ffn_w_gate_up": ffn_w_gate_up.astype(bf16), "ffn_w_down": ffn_w_down.astype(bf16),
    }


def kernel(x_prompt, x_sample, norm_mix_pre, norm_mix_post, norm_ffn_pre, norm_ffn_post, a_w_in, a_w_o, a_sinks, b_w_in, b_w_o, b_lambda_q1, b_lambda_k1, b_lambda_q2, b_lambda_k2, b_subln, ffn_w_gate_up, ffn_w_down):
    p = _prepare(norm_mix_pre, norm_mix_post, norm_ffn_pre, norm_ffn_post, a_w_in, a_w_o, a_sinks,
                 b_w_in, b_w_o, b_lambda_q1, b_lambda_k1, b_lambda_q2, b_lambda_k2, b_subln,
                 ffn_w_gate_up, ffn_w_down)
    return (_trunk(x_prompt, p), _trunk(x_sample, p))
```
